```python
import math
import jax, jax.numpy as jnp
from jax import lax
import numpy as np


D_MODEL = 1024
BATCH = 16
SEQ = 2048
DEPTH = 2

D_MIX = D_MODEL
M_WIDTH = D_MIX // 2
M_HEADDIM = 64
M_HEADS = M_WIDTH // M_HEADDIM
M_GROUPS = 2
M_HPG = M_HEADS // M_GROUPS
M_STATE = 128
M_CONV = 4
M_CHUNK = 128
M_CONV_DIM = M_WIDTH + 2 * M_GROUPS * M_STATE
M_PROJ = M_WIDTH + M_CONV_DIM + M_HEADS
S_WIDTH = D_MIX // 4
S_GROUP_CH = 16
S_GROUPS = S_WIDTH // S_GROUP_CH
S_STATE = 64
R_WIDTH = D_MIX - M_WIDTH - S_WIDTH
R_HEADDIM = 64
R_HEADS = R_WIDTH // R_HEADDIM
R_DECAY_LORA = 32
R_AAA_LORA = 32
R_GATE_LORA = 64
R_PROJ = 3 * R_WIDTH + R_DECAY_LORA + R_AAA_LORA + R_GATE_LORA
D_IN = M_PROJ + S_WIDTH + R_PROJ
D_FF = 2816
NORM_EPS = 1e-5
RWKV_GN_EPS = 64e-5
MACARON_WEIGHT = 0.5

kernel_name = 'hybrid_ssd_s5_rwkv7_macaron'


def rmsnorm(x, g, eps=NORM_EPS):
    xf = x.astype(jnp.float32)
    y = xf * lax.rsqrt(jnp.mean(xf * xf, axis=-1, keepdims=True) + eps)
    return y * g.astype(jnp.float32)


def swiglu(h, wg, wu, wd):
    return (jax.nn.silu(h @ wg) * (h @ wu)) @ wd


def token_shift(u):
    return jnp.pad(u[:, :-1], ((0, 0), (1, 0), (0, 0)))


def causal_dwconv(u, w, b):
    L = u.shape[1]
    up = jnp.pad(u, ((0, 0), (M_CONV - 1, 0), (0, 0)))
    out = b
    for j in range(M_CONV):
        out = out + up[:, j:j + L] * w[:, j]
    return out


def segsum(x):
    T = x.shape[-1]
    xr = jnp.broadcast_to(x[..., None], x.shape + (T,))
    xr = jnp.where(jnp.tril(jnp.ones((T, T), dtype=bool), -1), xr, 0.0)
    xs = jnp.cumsum(xr, axis=-2)
    return jnp.where(jnp.tril(jnp.ones((T, T), dtype=bool), 0), xs, -jnp.inf)


def ssd_mixer(u, A_log, dt_bias, conv_w, conv_b, D_skip, norm_w):
    bsz, L, _ = u.shape
    nc = L // M_CHUNK
    z, xbc, dt_raw = jnp.split(u, [M_WIDTH, M_WIDTH + M_CONV_DIM], axis=-1)
    xbc = jax.nn.silu(causal_dwconv(xbc, conv_w, conv_b))
    xs, Bm, Cm = jnp.split(xbc, [M_WIDTH, M_WIDTH + M_GROUPS * M_STATE], axis=-1)
    dt = jax.nn.softplus(dt_raw + dt_bias)
    A = -jnp.exp(A_log.astype(jnp.float32))
    xs = xs.reshape(bsz, nc, M_CHUNK, M_GROUPS, M_HPG, M_HEADDIM)
    Bm = Bm.reshape(bsz, nc, M_CHUNK, M_GROUPS, M_STATE)
    Cm = Cm.reshape(bsz, nc, M_CHUNK, M_GROUPS, M_STATE)
    dt = dt.reshape(bsz, nc, M_CHUNK, M_GROUPS, M_HPG)
    X = xs * dt[..., None]
    dA = jnp.transpose(dt * A.reshape(M_GROUPS, M_HPG), (0, 3, 4, 1, 2))
    A_cs = jnp.cumsum(dA, axis=-1)
    Lmat = jnp.exp(segsum(dA))
    CB = jnp.einsum('bclgn,bcsgn->bgcls', Cm, Bm)
    y_diag = jnp.einsum('bgrcls,bcsgrp->bclgrp', CB[:, :, None] * Lmat, X)
    decay_states = jnp.exp(A_cs[..., -1:] - A_cs)
    states = jnp.einsum('bclgn,bgrcl,bclgrp->bcgrpn', Bm, decay_states, X)
    A_last = jnp.pad(A_cs[..., -1], ((0, 0), (0, 0), (0, 0), (1, 0)))
    decay_chunk = jnp.exp(segsum(A_last))
    states = jnp.concatenate([jnp.zeros_like(states[:, :1]), states], axis=1)
    states = jnp.einsum('bgrzc,bcgrpn->bzgrpn', decay_chunk, states)[:, :-1]
    y_off = jnp.einsum('bclgn,bcgrpn,bgrcl->bclgrp', Cm, states, jnp.exp(A_cs))
    y = y_diag + y_off + xs * D_skip.reshape(M_GROUPS, M_HPG)[:, :, None]
    y = y.reshape(bsz, L, M_WIDTH)
    return rmsnorm(y * jax.nn.silu(z), norm_w)


def s5_mixer(u, A_re, A_im, B_re, B_im, C_re, C_im, log_dt, D_skip, glu_w, glu_b):
    bsz, L, _ = u.shape
    f32 = jnp.float32
    A_re, A_im = A_re.astype(f32), A_im.astype(f32)
    B_re, B_im = B_re.astype(f32), B_im.astype(f32)
    dt = jnp.exp(log_dt.astype(f32))[:, None]
    mag = jnp.exp(A_re * dt)
    abar_re, abar_im = mag * jnp.cos(A_im * dt), mag * jnp.sin(A_im * dt)
    den = A_re * A_re + A_im * A_im
    nr, ni = abar_re - 1.0, abar_im
    coef_re = (nr * A_re + ni * A_im) / den
    coef_im = (ni * A_re - nr * A_im) / den
    Bb_re = coef_re[..., None] * B_re - coef_im[..., None] * B_im
    Bb_im = coef_re[..., None] * B_im + coef_im[..., None] * B_re
    ug = u.reshape(bsz, L, S_GROUPS, S_GROUP_CH)
    bu_re = jnp.einsum('blgh,gph->blgp', ug, Bb_re)
    bu_im = jnp.einsum('blgh,gph->blgp', ug, Bb_im)
    a_re = jnp.broadcast_to(abar_re, bu_re.shape)
    a_im = jnp.broadcast_to(abar_im, bu_im.shape)

    def combine(e1, e2):
        a1r, a1i, b1r, b1i = e1
        a2r, a2i, b2r, b2i = e2
        return (a2r * a1r - a2i * a1i, a2r * a1i + a2i * a1r,
                a2r * b1r - a2i * b1i + b2r, a2r * b1i + a2i * b1r + b2i)

    _, _, xr, xi = lax.associative_scan(combine, (a_re, a_im, bu_re, bu_im), axis=1)
    y = (jnp.einsum('ghp,blgp->blgh', C_re, xr)
         - jnp.einsum('ghp,blgp->blgh', C_im, xi))
    y = y.reshape(bsz, L, S_WIDTH) + D_skip * u
    y = jax.nn.gelu(y)
    return y * jax.nn.sigmoid(y @ glu_w + glu_b)


def rwkv7_mixer(u, mu, w0, w2, a0, a2, g2, k_k, k_a, r_k, gn_w, gn_b):
    bsz, L, _ = u.shape
    u = u + (token_shift(u) - u) * mu
    r, k, v, wl, al, gl = jnp.split(
        u, [R_WIDTH, 2 * R_WIDTH, 3 * R_WIDTH, 3 * R_WIDTH + R_DECAY_LORA,
            3 * R_WIDTH + R_DECAY_LORA + R_AAA_LORA], axis=-1)
    w_log = -jax.nn.softplus(-(w0 + jnp.tanh(wl) @ w2)) - 0.5
    decay = jnp.exp(-jnp.exp(w_log))
    a = jax.nn.sigmoid(a0 + al @ a2)
    g = jax.nn.sigmoid(gl) @ g2
    heads = lambda t: t.reshape(bsz, L, R_HEADS, R_HEADDIM)
    kk = heads(k * k_k)
    kk = kk / jnp.maximum(jnp.sqrt(jnp.sum(kk * kk, axis=-1, keepdims=True)), 1e-12)
    k = k * (1.0 + (a - 1.0) * k_a)
    r_h, k_h, v_h, w_h, a_h = heads(r), heads(k), heads(v), heads(decay), heads(a)

    def step(S, inp):
        rt, wt, kt, vt, kkt, at = inp
        sa = jnp.einsum('bhvk,bhk->bhv', S, -kkt)
        S = (S * wt[:, :, None, :] + sa[..., None] * (kkt * at)[:, :, None, :]
             + vt[..., None] * kt[:, :, None, :])
        return S, jnp.einsum('bhvk,bhk->bhv', S, rt)

    S0 = jnp.zeros((bsz, R_HEADS, R_HEADDIM, R_HEADDIM), jnp.float32)
    xs = tuple(jnp.moveaxis(t, 1, 0) for t in (r_h, w_h, k_h, v_h, kk, a_h))
    _, y = lax.scan(step, S0, xs)
    y = jnp.moveaxis(y, 0, 1)
    mean = jnp.mean(y, axis=-1, keepdims=True)
    var = jnp.mean(jnp.square(y - mean), axis=-1, keepdims=True)
    y = ((y - mean) * lax.rsqrt(var + RWKV_GN_EPS)).reshape(bsz, L, R_WIDTH) * gn_w + gn_b
    bonus = jnp.sum(r_h * k_h * r_k, axis=-1, keepdims=True) * v_h
    y = y + bonus.reshape(bsz, L, R_WIDTH)
    return y * g


def setup_inputs(seed: int = 0) -> dict:
    key = jax.random.key(seed)
    ks = iter(jax.random.split(key, 64))
    f32 = jnp.float32
    nrm = lambda shape, s: s * jax.random.normal(next(ks), shape, f32)
    gain = lambda shape: 1.0 + 0.02 * jax.random.normal(next(ks), shape, f32)
    unif = lambda shape, lo, hi: jax.random.uniform(next(ks), shape, f32, lo, hi)
    Dn = DEPTH
    inp = {}
    inp['x'] = nrm((BATCH, SEQ, D_MODEL), 1.0)
    inp['ffn1_norm'] = gain((Dn, D_MODEL))
    inp['ffn1_wg'] = nrm((Dn, D_MODEL, D_FF), D_MODEL ** -0.5)
    inp['ffn1_wu'] = nrm((Dn, D_MODEL, D_FF), D_MODEL ** -0.5)
    inp['ffn1_wd'] = nrm((Dn, D_FF, D_MODEL), D_FF ** -0.5)
    inp['mix_norm'] = gain((Dn, D_MODEL))
    inp['w_in'] = nrm((Dn, D_MODEL, D_IN), D_MODEL ** -0.5)
    inp['w_out'] = nrm((Dn, D_MIX, D_MODEL), D_MIX ** -0.5)
    inp['m_A_log'] = jnp.log(unif((Dn, M_HEADS), 1.0, 16.0))
    dt0 = jnp.exp(unif((Dn, M_HEADS), math.log(1e-3), math.log(1e-1)))
    inp['m_dt_bias'] = dt0 + jnp.log(-jnp.expm1(-dt0))
    inp['m_conv_w'] = nrm((Dn, M_CONV_DIM, M_CONV), M_CONV ** -0.5)
    inp['m_conv_b'] = nrm((Dn, M_CONV_DIM), 0.02)
    inp['m_D'] = gain((Dn, M_HEADS))
    inp['m_norm_w'] = gain((Dn, M_WIDTH))
    inp['s_A_re'] = -0.5 + nrm((Dn, S_GROUPS, S_STATE), 0.01)
    inp['s_A_im'] = math.pi * jnp.arange(S_STATE, dtype=f32) + nrm((Dn, S_GROUPS, S_STATE), 0.01)
    inp['s_B_re'] = nrm((Dn, S_GROUPS, S_STATE, S_GROUP_CH), (2 * S_GROUP_CH) ** -0.5)
    inp['s_B_im'] = nrm((Dn, S_GROUPS, S_STATE, S_GROUP_CH), (2 * S_GROUP_CH) ** -0.5)
    inp['s_C_re'] = nrm((Dn, S_GROUPS, S_GROUP_CH, S_STATE), S_STATE ** -0.5)
    inp['s_C_im'] = nrm((Dn, S_GROUPS, S_GROUP_CH, S_STATE), S_STATE ** -0.5)
    inp['s_log_dt'] = unif((Dn, S_GROUPS), math.log(1e-3), math.log(1e-1))
    inp['s_D'] = nrm((Dn, S_WIDTH), 1.0)
    inp['s_glu_w'] = nrm((Dn, S_WIDTH, S_WIDTH), S_WIDTH ** -0.5)
    inp['s_glu_b'] = nrm((Dn, S_WIDTH), 0.02)
    inp['r_mu'] = unif((Dn, R_PROJ), 0.0, 1.0)
    inp['r_w0'] = unif((Dn, R_WIDTH), -6.0, 1.0)
    inp['r_w2'] = nrm((Dn, R_DECAY_LORA, R_WIDTH), 0.1 * R_DECAY_LORA ** -0.5)
    inp['r_a0'] = nrm((Dn, R_WIDTH), 0.1)
    inp['r_a2'] = nrm((Dn, R_AAA_LORA, R_WIDTH), 0.1 * R_AAA_LORA ** -0.5)
    inp['r_g2'] = nrm((Dn, R_GATE_LORA, R_WIDTH), R_GATE_LORA ** -0.5)
    inp['r_k_k'] = 0.85 + nrm((Dn, R_WIDTH), 0.02)
    inp['r_k_a'] = gain((Dn, R_WIDTH))
    inp['r_r_k'] = nrm((Dn, R_HEADS, R_HEADDIM), 0.1)
    inp['r_gn_w'] = gain((Dn, R_WIDTH))
    inp['r_gn_b'] = nrm((Dn, R_WIDTH), 0.02)
    inp['ffn2_norm'] = gain((Dn, D_MODEL))
    inp['ffn2_wg'] = nrm((Dn, D_MODEL, D_FF), D_MODEL ** -0.5)
    inp['ffn2_wu'] = nrm((Dn, D_MODEL, D_FF), D_MODEL ** -0.5)
    inp['ffn2_wd'] = nrm((Dn, D_FF, D_MODEL), D_FF ** -0.5)
    inp['final_norm'] = gain((D_MODEL,))
    return inp


def reference(x, ffn1_norm, ffn1_wg, ffn1_wu, ffn1_wd, mix_norm, w_in, w_out,
              m_A_log, m_dt_bias, m_conv_w, m_conv_b, m_D, m_norm_w,
              s_A_re, s_A_im, s_B_re, s_B_im, s_C_re, s_C_im, s_log_dt, s_D, s_glu_w, s_glu_b,
              r_mu, r_w0, r_w2, r_a0, r_a2, r_g2, r_k_k, r_k_a, r_r_k, r_gn_w, r_gn_b,
              ffn2_norm, ffn2_wg, ffn2_wu, ffn2_wd, final_norm):
    for i in range(DEPTH):
        h = rmsnorm(x, ffn1_norm[i])
        x = x + (MACARON_WEIGHT * swiglu(h, ffn1_wg[i], ffn1_wu[i], ffn1_wd[i])).astype(x.dtype)
        h = rmsnorm(x, mix_norm[i])
        u = h @ w_in[i]
        u_m, u_s, u_r = jnp.split(u, [M_PROJ, M_PROJ + S_WIDTH], axis=-1)
        y_m = ssd_mixer(u_m, m_A_log[i], m_dt_bias[i], m_conv_w[i], m_conv_b[i], m_D[i], m_norm_w[i])
        y_s = s5_mixer(u_s, s_A_re[i], s_A_im[i], s_B_re[i], s_B_im[i], s_C_re[i], s_C_im[i],
                       s_log_dt[i], s_D[i], s_glu_w[i], s_glu_b[i])
        y_r = rwkv7_mixer(u_r, r_mu[i], r_w0[i], r_w2[i], r_a0[i], r_a2[i], r_g2[i],
                          r_k_k[i], r_k_a[i], r_r_k[i], r_gn_w[i], r_gn_b[i])
        y = jnp.concatenate([y_m, y_s, y_r], axis=-1)
        x = x + (y @ w_out[i]).astype(x.dtype)
        h = rmsnorm(x, ffn2_norm[i])
        x = x + (MACARON_WEIGHT * swiglu(h, ffn2_wg[i], ffn2_wu[i], ffn2_wd[i])).astype(x.dtype)
    return rmsnorm(x, final_norm).astype(x.dtype)
```

```python
import functools
import math

import jax
import jax.numpy as jnp
from jax import lax
from jax.experimental import pallas as pl
from jax.experimental.pallas import tpu as pltpu

F32 = jnp.float32
BF16 = jnp.bfloat16

NORM_EPS = 1e-5
RWKV_GN_EPS = 64e-5
MACARON_WEIGHT = 0.5

LANES = 128
SUBLANES = 8
VMEM_LIMIT_BYTES = 56 * 1024 * 1024

D_MODEL = 1024
D_FF = 2816
M_WIDTH = 512
M_HEADDIM = 64
M_HEADS = 8
M_GROUPS = 2
M_STATE = 128
M_CONV = 4
M_CONV_DIM = 1024
S_WIDTH = 256
S_GROUPS = 16
S_GROUP_CH = 16
S_STATE = 64
R_WIDTH = 256
R_HEADS = 4
R_HEADDIM = 64
R_DECAY_LORA = 32
R_AAA_LORA = 32
R_GATE_LORA = 64
R_PROJ = 896
DT_PAD = LANES
U_COLS = M_WIDTH + M_CONV_DIM + DT_PAD + S_WIDTH + R_PROJ

ROW_TILE = 512
FF_CHUNK = 256
SSD_CHUNK = 128
S5_CHUNK = 128
S5_LOG_CHUNK = 7
RWKV_CHUNK = 64
CONV_HALO = SUBLANES


def _mm(a, b):
    return jnp.dot(a.astype(BF16), b.astype(BF16), preferred_element_type=F32)


def _mm_nt(a, b):
    return lax.dot_general(a.astype(BF16), b.astype(BF16), (((1,), (1,)), ((), ())),
                           preferred_element_type=F32)


def _mm_tn(a, b):
    return lax.dot_general(a.astype(BF16), b.astype(BF16), (((0,), (0,)), ((), ())),
                           preferred_element_type=F32)


def _split_mm(x, m, passes, left=False):
    acc = None
    rem = x
    for _ in range(passes):
        hi = rem.astype(BF16)
        p = (jnp.dot(m, hi, preferred_element_type=F32) if left
             else jnp.dot(hi, m, preferred_element_type=F32))
        acc = p if acc is None else acc + p
        rem = rem - hi.astype(F32)
    return acc


def _rmsnorm(x, g, eps=NORM_EPS):
    return x * lax.rsqrt(jnp.mean(x * x, axis=-1, keepdims=True) + eps) * g


def _sigmoid(x):
    return 1.0 / (1.0 + jnp.exp(-x))


def _silu(x):
    return x * _sigmoid(x)


def _softplus(x):
    return jnp.maximum(x, 0.0) + jnp.log(1.0 + jnp.exp(-jnp.abs(x)))


def _iota(shape, dim):
    return lax.broadcasted_iota(jnp.int32, shape, dim)


def _tril_ones(n, dtype=BF16):
    return jnp.where(_iota((n, n), 0) >= _iota((n, n), 1), 1.0, 0.0).astype(dtype)


def _shift_rows(x, s):
    n = x.shape[0]
    if s % SUBLANES == 0:
        return jnp.concatenate([jnp.zeros((s, x.shape[1]), x.dtype), x[:n - s]], axis=0)
    rolled = pltpu.roll(x, s, axis=0)
    return jnp.where(_iota(x.shape, 0) >= s, rolled, 0.0)


def _resident(shape):
    nd = len(shape)
    return pl.BlockSpec(shape, lambda *_: (0,) * nd, pipeline_mode=pl.Buffered(1))


def _params(sem):
    return pltpu.CompilerParams(dimension_semantics=sem, vmem_limit_bytes=VMEM_LIMIT_BYTES)


def _ffn_kernel(x_ref, g_ref, wg_ref, wu_ref, wd_ref, o_ref, act_ref):
    x = x_ref[...]
    h = _rmsnorm(x, g_ref[...]).astype(BF16)
    for j in range(D_FF // FF_CHUNK):
        sl = slice(j * FF_CHUNK, (j + 1) * FF_CHUNK)
        gate = jnp.dot(h, wg_ref[:, sl], preferred_element_type=F32)
        up = jnp.dot(h, wu_ref[:, sl], preferred_element_type=F32)
        act_ref[:, sl] = (_silu(gate) * up).astype(BF16)
    y = jnp.dot(act_ref[...], wd_ref[...], preferred_element_type=F32)
    o_ref[...] = x + MACARON_WEIGHT * y


def _ffn(x, g, wg, wu, wd):
    n = x.shape[0]
    row = pl.BlockSpec((ROW_TILE, D_MODEL), lambda i: (i, 0))
    return pl.pallas_call(
        _ffn_kernel,
        out_shape=jax.ShapeDtypeStruct((n, D_MODEL), F32),
        grid=(n // ROW_TILE,),
        in_specs=[row, _resident((1, D_MODEL)), _resident((D_MODEL, D_FF)),
                  _resident((D_MODEL, D_FF)), _resident((D_FF, D_MODEL))],
        out_specs=row,
        scratch_shapes=[pltpu.VMEM((ROW_TILE, D_FF), BF16)],
        compiler_params=_params(("parallel",)),
        name="ffn",
    )(x, g, wg, wu, wd)


def _inproj_kernel(x_ref, g_ref, w_ref, z_ref, xbc_ref, dt_ref, us_ref, ur_ref):
    h = _rmsnorm(x_ref[...], g_ref[...]).astype(BF16)
    o = 0
    for ref in (z_ref, xbc_ref, dt_ref, us_ref, ur_ref):
        w = ref.shape[-1]
        ref[...] = jnp.dot(h, w_ref[:, o:o + w], preferred_element_type=F32)
        o += w


def _inproj(x, g, w):
    n = x.shape[0]
    widths = (M_WIDTH, M_CONV_DIM, DT_PAD, S_WIDTH, R_PROJ)
    return pl.pallas_call(
        _inproj_kernel,
        out_shape=[jax.ShapeDtypeStruct((n, c), F32) for c in widths],
        grid=(n // ROW_TILE,),
        in_specs=[pl.BlockSpec((ROW_TILE, D_MODEL), lambda i: (i, 0)),
                  _resident((1, D_MODEL)), _resident((D_MODEL, U_COLS))],
        out_specs=[pl.BlockSpec((ROW_TILE, c), lambda i: (i, 0)) for c in widths],
        compiler_params=_params(("parallel",)),
        name="inproj",
    )(x, g, w)


def _outproj_kernel(x_ref, ym_ref, ys_ref, yr_ref, w_ref, o_ref):
    y = jnp.dot(ym_ref[...].astype(BF16), w_ref[0:M_WIDTH, :], preferred_element_type=F32)
    y += jnp.dot(ys_ref[...].astype(BF16), w_ref[M_WIDTH:M_WIDTH + S_WIDTH, :],
                 preferred_element_type=F32)
    y += jnp.dot(yr_ref[...].astype(BF16), w_ref[M_WIDTH + S_WIDTH:, :],
                 preferred_element_type=F32)
    o_ref[...] = x_ref[...] + y


def _outproj(x, ym, ys, yr, w):
    n = x.shape[0]
    rows = lambda c: pl.BlockSpec((ROW_TILE, c), lambda i: (i, 0))
    return pl.pallas_call(
        _outproj_kernel,
        out_shape=jax.ShapeDtypeStruct((n, D_MODEL), F32),
        grid=(n // ROW_TILE,),
        in_specs=[rows(D_MODEL), rows(M_WIDTH), rows(S_WIDTH), rows(R_WIDTH),
                  _resident((D_MODEL, D_MODEL))],
        out_specs=rows(D_MODEL),
        compiler_params=_params(("parallel",)),
        name="outproj",
    )(x, ym, ys, yr, w)


def _final_norm_kernel(x_ref, g_ref, o_ref):
    o_ref[...] = _rmsnorm(x_ref[...], g_ref[...])


def _final_norm(x, g):
    n = x.shape[0]
    row = pl.BlockSpec((ROW_TILE, D_MODEL), lambda i: (i, 0))
    return pl.pallas_call(
        _final_norm_kernel,
        out_shape=jax.ShapeDtypeStruct((n, D_MODEL), F32),
        grid=(n // ROW_TILE,),
        in_specs=[row, _resident((1, D_MODEL))],
        out_specs=row,
        compiler_params=_params(("parallel",)),
        name="final_norm",
    )(x, g)


def _ssd_kernel(z_ref, xbc_ref, dt_ref, convw_ref, convb_ref, dtb_ref, a_ref, dskip_ref,
                normw_ref, expand_ref, y_ref, win_ref, state_ref):
    T = SSD_CHUNK
    GN = M_STATE
    GW = M_WIDTH // M_GROUPS

    @pl.when(pl.program_id(1) == 0)
    def _():
        win_ref[0:CONV_HALO, :] = jnp.zeros((CONV_HALO, M_CONV_DIM), F32)
        state_ref[...] = jnp.zeros_like(state_ref)

    win_ref[CONV_HALO:CONV_HALO + T, :] = xbc_ref[...]
    acc = jnp.broadcast_to(convb_ref[...], (T, M_CONV_DIM))
    for j in range(M_CONV):
        o = CONV_HALO - (M_CONV - 1) + j
        acc = acc + win_ref[o:o + T, :] * convw_ref[j:j + 1, :]
    win_ref[0:CONV_HALO, :] = win_ref[T:T + CONV_HALO, :]
    xa = _silu(acc)
    xs = xa[:, 0:M_WIDTH]
    bm = xa[:, M_WIDTH:M_WIDTH + M_GROUPS * GN]
    cm = xa[:, M_WIDTH + M_GROUPS * GN:]

    dt = _softplus(dt_ref[...] + dtb_ref[...])
    da = dt * a_ref[...]
    cs = _split_mm(da, _tril_ones(T), 3, left=True)
    cs_t = cs.T
    expand = expand_ref[...]
    dt_x = _split_mm(dt, expand, 2)
    cs_x = _split_mm(cs, expand, 3)
    cs_last_x = cs_x[T - 1:T, :]
    x_dt = xs * dt_x
    x_dec = x_dt * jnp.exp(cs_last_x - cs_x)
    ecs_x = jnp.exp(cs_x)
    state_decay = jnp.exp(cs_last_x)

    causal = _iota((T, T), 0) >= _iota((T, T), 1)
    lane_lo = _iota((T, LANES), 1) < M_HEADDIM
    y_parts = []
    for g in range(M_GROUPS):
        cg = cm[:, g * GN:(g + 1) * GN]
        bg = bm[:, g * GN:(g + 1) * GN]
        cb = _mm_nt(cg, bg)
        for q in range(GW // LANES):
            col = g * GW + q * LANES
            xq = x_dt[:, col:col + LANES]
            outs = []
            for r in range(LANES // M_HEADDIM):
                h = col // M_HEADDIM + r
                seg = cs[:, h:h + 1] - cs_t[h:h + 1, :]
                lmat = jnp.where(causal, jnp.exp(seg), 0.0)
                outs.append(_mm(cb * lmat, xq))
            y_parts.append(jnp.where(lane_lo, outs[0], outs[1]))
    y_diag = jnp.concatenate(y_parts, axis=1)

    y_off_parts = []
    for g in range(M_GROUPS):
        cg = cm[:, g * GN:(g + 1) * GN]
        bg = bm[:, g * GN:(g + 1) * GN]
        st = state_ref[g * GN:(g + 1) * GN, :]
        y_off_parts.append(_mm(cg, st))
        state_ref[g * GN:(g + 1) * GN, :] = (
            st * state_decay[:, g * GW:(g + 1) * GW] + _mm_tn(bg, x_dec[:, g * GW:(g + 1) * GW]))
    y_off = jnp.concatenate(y_off_parts, axis=1) * ecs_x

    y = y_diag + y_off + xs * dskip_ref[...]
    y = y * _silu(z_ref[...])
    y_ref[...] = _rmsnorm(y, normw_ref[...])


def _ssd(z, xbc, dt, convw, convb, dtb, a, dskip, normw, expand):
    bsz, seq, _ = z.shape
    T = SSD_CHUNK
    tok = lambda c: pl.BlockSpec((None, T, c), lambda b, i: (b, i, 0))
    return pl.pallas_call(
        _ssd_kernel,
        out_shape=jax.ShapeDtypeStruct((bsz, seq, M_WIDTH), F32),
        grid=(bsz, seq // T),
        in_specs=[tok(M_WIDTH), tok(M_CONV_DIM), tok(DT_PAD),
                  _resident((M_CONV, M_CONV_DIM)), _resident((1, M_CONV_DIM)),
                  _resident((1, DT_PAD)), _resident((1, DT_PAD)), _resident((1, M_WIDTH)),
                  _resident((1, M_WIDTH)), _resident((DT_PAD, M_WIDTH))],
        out_specs=tok(M_WIDTH),
        scratch_shapes=[pltpu.VMEM((T + CONV_HALO, M_CONV_DIM), F32),
                        pltpu.VMEM((M_GROUPS * M_STATE, M_WIDTH // M_GROUPS), F32)],
        compiler_params=_params(("parallel", "arbitrary")),
        name="ssd",
    )(z, xbc, dt, convw, convb, dtb, a, dskip, normw, expand)


def _s5_kernel(u_ref, wb_ref, apr_ref, api_ref, wcr_ref, wci_ref, dskip_ref, gluw_ref,
               glub_ref, y_ref, carry_ref):
    T = S5_CHUNK
    NS = S_GROUPS * S_STATE

    @pl.when(pl.program_id(1) == 0)
    def _():
        carry_ref[...] = jnp.zeros_like(carry_ref)

    u = u_ref[...]
    bu = _mm(u, wb_ref[...])
    cr = carry_ref[0:1, 0:NS]
    ci = carry_ref[0:1, NS:]
    ar = apr_ref[0:1, :]
    ai = api_ref[0:1, :]
    first = _iota((T, NS), 0) == 0
    xr = bu[:, 0:NS] + jnp.where(first, ar * cr - ai * ci, 0.0)
    xi = bu[:, NS:] + jnp.where(first, ar * ci + ai * cr, 0.0)
    for j in range(S5_LOG_CHUNK):
        s = 1 << j
        sr = _shift_rows(xr, s)
        si = _shift_rows(xi, s)
        pr = apr_ref[j:j + 1, :]
        pi = api_ref[j:j + 1, :]
        xr, xi = xr + (pr * sr - pi * si), xi + (pr * si + pi * sr)
    carry_ref[:, 0:NS] = jnp.broadcast_to(xr[T - 1:T, :], (SUBLANES, NS))
    carry_ref[:, NS:] = jnp.broadcast_to(xi[T - 1:T, :], (SUBLANES, NS))

    y = _mm(xr, wcr_ref[...]) + _mm(xi, wci_ref[...])
    y = y + dskip_ref[...] * u
    y = 0.5 * y * (1.0 + jnp.tanh(math.sqrt(2.0 / math.pi) * (y + 0.044715 * (y * y * y))))
    y_ref[...] = y * _sigmoid(_mm(y, gluw_ref[...]) + glub_ref[...])


def _s5(u, wb, apr, api, wcr, wci, dskip, gluw, glub):
    bsz, seq, _ = u.shape
    T = S5_CHUNK
    NS = S_GROUPS * S_STATE
    tok = pl.BlockSpec((None, T, S_WIDTH), lambda b, i: (b, i, 0))
    return pl.pallas_call(
        _s5_kernel,
        out_shape=jax.ShapeDtypeStruct((bsz, seq, S_WIDTH), F32),
        grid=(bsz, seq // T),
        in_specs=[tok, _resident((S_WIDTH, 2 * NS)), _resident((SUBLANES, NS)),
                  _resident((SUBLANES, NS)), _resident((NS, S_WIDTH)),
                  _resident((NS, S_WIDTH)), _resident((1, S_WIDTH)),
                  _resident((S_WIDTH, S_WIDTH)), _resident((1, S_WIDTH))],
        out_specs=tok,
        scratch_shapes=[pltpu.VMEM((SUBLANES, 2 * NS), F32)],
        compiler_params=_params(("parallel", "arbitrary")),
        name="s5",
    )(u, wb, apr, api, wcr, wci, dskip, gluw, glub)


def _rwkv_kernel(u_ref, mu_ref, w0_ref, w2_ref, a0_ref, a2_ref, g2_ref, kk_ref, ka_ref,
                 rk_ref, gnw_ref, gnb_ref, y_ref, prev_ref, state_ref):
    T = RWKV_CHUNK
    W = R_WIDTH
    HT = R_HEADS * T

    @pl.when(pl.program_id(1) == 0)
    def _():
        prev_ref[...] = jnp.zeros_like(prev_ref)
        state_ref[...] = jnp.zeros_like(state_ref)

    u = u_ref[...]
    rolled = pltpu.roll(u, 1, axis=0)
    shifted = jnp.where(_iota(u.shape, 0) == 0, prev_ref[0:1, :], rolled)
    prev_ref[...] = jnp.broadcast_to(u[T - 1:T, :], prev_ref.shape)
    u = u + (shifted - u) * mu_ref[...]
    r = u[:, 0:W]
    k = u[:, W:2 * W]
    v = u[:, 2 * W:3 * W]
    lora = u[:, 3 * W:]

    w_log = -_softplus(-(w0_ref[...] + _mm(jnp.tanh(lora), w2_ref[...]))) - 0.5
    logw = -jnp.exp(w_log)
    a = _sigmoid(a0_ref[...] + _mm(lora, a2_ref[...]))
    gate = _mm(_sigmoid(lora), g2_ref[...])

    head_of_lane = _iota((W, W), 0) // R_HEADDIM == _iota((W, W), 1) // R_HEADDIM
    head_ones = jnp.where(head_of_lane, 1.0, 0.0).astype(BF16)

    kk = k * kk_ref[...]
    kk = kk / jnp.maximum(jnp.sqrt(_split_mm(kk * kk, head_ones, 2)), 1e-12)
    k = k * (1.0 + (a - 1.0) * ka_ref[...])

    cum = _split_mm(logw, _tril_ones(T), 3, left=True)
    cum_end = cum[T - 1:T, :]
    g_inv = jnp.exp(-cum)
    g_tail = jnp.exp(cum_end - cum)
    kka = kk * a
    a_t = -kk * jnp.exp(cum - logw)
    r_t = r * jnp.exp(cum)
    b_t = kka * g_inv
    k_t = k * g_inv
    b_h = kka * g_tail
    k_h = k * g_tail

    lane_head = _iota((T, W), 1) // R_HEADDIM

    def stack(x):
        return jnp.concatenate([jnp.where(lane_head == h, x, 0.0) for h in range(R_HEADS)],
                               axis=0)

    def rep(x):
        return jnp.concatenate([x] * R_HEADS, axis=0)

    def unstack(x):
        out = x[0:T]
        for h in range(1, R_HEADS):
            out = out + x[h * T:(h + 1) * T]
        return out

    row = _iota((HT, HT), 0)
    col = _iota((HT, HT), 1)
    same_head = row // T == col // T
    strict = same_head & (row % T > col % T)
    lower = same_head & (row % T >= col % T)

    sa = stack(a_t)
    sr = stack(r_t)
    sv = stack(v)
    rb = rep(b_t)
    rk = rep(k_t)
    n_ab = jnp.where(strict, _mm_nt(sa, rb), 0.0)
    m_ak = jnp.where(strict, _mm_nt(sa, rk), 0.0)
    m_rb = jnp.where(lower, _mm_nt(sr, rb), 0.0)
    m_rk = jnp.where(lower, _mm_nt(sr, rk), 0.0)

    x = jnp.concatenate([sa, _mm(m_ak, sv)], axis=1)
    p = n_ab
    steps = int(math.log2(T))
    for j in range(steps):
        x = x + _mm(p, x)
        if j + 1 < steps:
            p = _mm(p, p)
    w_s = x[:, 0:W]
    u0_s = x[:, W:]

    q = unstack(sr + _mm(m_rb, w_s))
    y0 = unstack(_mm(m_rk, sv) + _mm(m_rb, u0_s))
    w_u = unstack(w_s)
    u0 = unstack(u0_s)

    s0 = state_ref[...]
    y = _mm_nt(q, s0) + y0
    p_t = jnp.where(head_of_lane, _mm_tn(w_u, b_h), 0.0)
    z_t = jnp.where(head_of_lane,
                    _mm_tn(jnp.concatenate([v, u0], axis=0),
                           jnp.concatenate([k_h, b_h], axis=0)), 0.0)
    state_ref[...] = s0 * jnp.exp(cum_end) + _mm(s0, p_t) + z_t

    inv_n = 1.0 / R_HEADDIM
    mean = _split_mm(y, head_ones, 2) * inv_n
    yc = y - mean
    var = _split_mm(yc * yc, head_ones, 2) * inv_n
    yn = yc * lax.rsqrt(var + RWKV_GN_EPS) * gnw_ref[...] + gnb_ref[...]
    bonus = _split_mm(r * k * rk_ref[...], head_ones, 2) * v
    y_ref[...] = (yn + bonus) * gate


def _rwkv(u, mu, w0, w2, a0, a2, g2, kk, ka, rk, gnw, gnb):
    bsz, seq, _ = u.shape
    T = RWKV_CHUNK
    vec = _resident((1, R_WIDTH))
    lora = _resident((R_PROJ - 3 * R_WIDTH, R_WIDTH))
    return pl.pallas_call(
        _rwkv_kernel,
        out_shape=jax.ShapeDtypeStruct((bsz, seq, R_WIDTH), F32),
        grid=(bsz, seq // T),
        in_specs=[pl.BlockSpec((None, T, R_PROJ), lambda b, i: (b, i, 0)),
                  _resident((1, R_PROJ)), vec, lora, vec, lora, lora, vec, vec, vec, vec, vec],
        out_specs=pl.BlockSpec((None, T, R_WIDTH), lambda b, i: (b, i, 0)),
        scratch_shapes=[pltpu.VMEM((SUBLANES, R_PROJ), F32),
                        pltpu.VMEM((R_WIDTH, R_WIDTH), F32)],
        compiler_params=_params(("parallel", "arbitrary")),
        name="rwkv7",
    )(u, mu, w0, w2, a0, a2, g2, kk, ka, rk, gnw, gnb)


def _pad_cols(a, n):
    return jnp.pad(a, ((0, 0), (0, n - a.shape[1])))


def _s5_discretize(A_re, A_im, B_re, B_im, C_re, C_im, log_dt):
    dt = jnp.exp(log_dt)[:, None]
    mag = jnp.exp(A_re * dt)
    abar_re, abar_im = mag * jnp.cos(A_im * dt), mag * jnp.sin(A_im * dt)
    den = A_re * A_re + A_im * A_im
    nr, ni = abar_re - 1.0, abar_im
    coef_re = (nr * A_re + ni * A_im) / den
    coef_im = (ni * A_re - nr * A_im) / den
    bb_re = coef_re[..., None] * B_re - coef_im[..., None] * B_im
    bb_im = coef_re[..., None] * B_im + coef_im[..., None] * B_re
    eye = jnp.eye(S_GROUPS, dtype=F32)
    ns = S_GROUPS * S_STATE
    in_bd = lambda m: jnp.einsum('gph,gk->ghkp', m, eye).reshape(S_WIDTH, ns)
    out_bd = lambda m: jnp.einsum('ghp,gk->gpkh', m, eye).reshape(ns, S_WIDTH)
    wb = jnp.concatenate([in_bd(bb_re), in_bd(bb_im)], axis=1).astype(BF16)
    wcr = out_bd(C_re).astype(BF16)
    wci = (-out_bd(C_im)).astype(BF16)
    pr, pi = abar_re.reshape(1, ns), abar_im.reshape(1, ns)
    prs, pis = [pr], [pi]
    for _ in range(SUBLANES - 1):
        pr, pi = pr * pr - pi * pi, 2.0 * pr * pi
        prs.append(pr)
        pis.append(pi)
    return wb, jnp.concatenate(prs, axis=0), jnp.concatenate(pis, axis=0), wcr, wci


def _layer_params(i, p):
    row = lambda a: a.reshape(1, -1).astype(F32)
    w_in = p['w_in'][i]
    m_proj = M_WIDTH + M_CONV_DIM
    w_u = jnp.concatenate([
        w_in[:, :m_proj],
        _pad_cols(w_in[:, m_proj:m_proj + M_HEADS], DT_PAD),
        w_in[:, m_proj + M_HEADS:]], axis=1).astype(BF16)
    head_lanes = jnp.repeat(jnp.eye(M_HEADS, dtype=F32), M_HEADDIM, axis=1)
    lora_rows = R_PROJ - 3 * R_WIDTH
    lora_pad = lambda m, o: jnp.pad(m, ((o, lora_rows - o - m.shape[0]), (0, 0))).astype(BF16)
    return dict(
        ffn1=(row(p['ffn1_norm'][i]), p['ffn1_wg'][i].astype(BF16), p['ffn1_wu'][i].astype(BF16),
              p['ffn1_wd'][i].astype(BF16)),
        ffn2=(row(p['ffn2_norm'][i]), p['ffn2_wg'][i].astype(BF16), p['ffn2_wu'][i].astype(BF16),
              p['ffn2_wd'][i].astype(BF16)),
        mix_norm=row(p['mix_norm'][i]),
        w_u=w_u,
        w_out=p['w_out'][i].astype(BF16),
        ssd=(p['m_conv_w'][i].T.astype(F32), row(p['m_conv_b'][i]),
             _pad_cols(row(p['m_dt_bias'][i]), DT_PAD),
             _pad_cols(row(-jnp.exp(p['m_A_log'][i].astype(F32))), DT_PAD),
             row(jnp.repeat(p['m_D'][i], M_HEADDIM)), row(p['m_norm_w'][i]),
             jnp.pad(head_lanes, ((0, DT_PAD - M_HEADS), (0, 0))).astype(BF16)),
        s5=_s5_discretize(p['s_A_re'][i].astype(F32), p['s_A_im'][i].astype(F32),
                          p['s_B_re'][i].astype(F32), p['s_B_im'][i].astype(F32),
                          p['s_C_re'][i], p['s_C_im'][i], p['s_log_dt'][i].astype(F32))
        + (row(p['s_D'][i]), p['s_glu_w'][i].astype(BF16), row(p['s_glu_b'][i])),
        rwkv=(row(p['r_mu'][i]), row(p['r_w0'][i]), lora_pad(p['r_w2'][i], 0),
              row(p['r_a0'][i]), lora_pad(p['r_a2'][i], R_DECAY_LORA),
              lora_pad(p['r_g2'][i], R_DECAY_LORA + R_AAA_LORA),
              row(p['r_k_k'][i]), row(p['r_k_a'][i]), row(p['r_r_k'][i]),
              row(p['r_gn_w'][i]), row(p['r_gn_b'][i])),
    )


def kernel(x, ffn1_norm, ffn1_wg, ffn1_wu, ffn1_wd, mix_norm, w_in, w_out, m_A_log, m_dt_bias, m_conv_w, m_conv_b, m_D, m_norm_w, s_A_re, s_A_im, s_B_re, s_B_im, s_C_re, s_C_im, s_log_dt, s_D, s_glu_w, s_glu_b, r_mu, r_w0, r_w2, r_a0, r_a2, r_g2, r_k_k, r_k_a, r_r_k, r_gn_w, r_gn_b, ffn2_norm, ffn2_wg, ffn2_wu, ffn2_wd, final_norm):
    p = dict(ffn1_norm=ffn1_norm, ffn1_wg=ffn1_wg, ffn1_wu=ffn1_wu, ffn1_wd=ffn1_wd,
             mix_norm=mix_norm, w_in=w_in, w_out=w_out, m_A_log=m_A_log, m_dt_bias=m_dt_bias,
             m_conv_w=m_conv_w, m_conv_b=m_conv_b, m_D=m_D, m_norm_w=m_norm_w,
             s_A_re=s_A_re, s_A_im=s_A_im, s_B_re=s_B_re, s_B_im=s_B_im, s_C_re=s_C_re,
             s_C_im=s_C_im, s_log_dt=s_log_dt, s_D=s_D, s_glu_w=s_glu_w, s_glu_b=s_glu_b,
             r_mu=r_mu, r_w0=r_w0, r_w2=r_w2, r_a0=r_a0, r_a2=r_a2, r_g2=r_g2, r_k_k=r_k_k,
             r_k_a=r_k_a, r_r_k=r_r_k, r_gn_w=r_gn_w, r_gn_b=r_gn_b, ffn2_norm=ffn2_norm,
             ffn2_wg=ffn2_wg, ffn2_wu=ffn2_wu, ffn2_wd=ffn2_wd)
    bsz, seq, d = x.shape
    assert d == D_MODEL and (bsz * seq) % ROW_TILE == 0
    assert seq % SSD_CHUNK == 0 and seq % S5_CHUNK == 0 and seq % RWKV_CHUNK == 0
    n = bsz * seq
    depth = w_in.shape[0]
    xf = x.reshape(n, d).astype(F32)
    for i in range(depth):
        lp = _layer_params(i, p)
        xf = _ffn(xf, *lp['ffn1'])
        z, xbc, dt, us, ur = _inproj(xf, lp['mix_norm'], lp['w_u'])
        tok = lambda a: a.reshape(bsz, seq, a.shape[-1])
        y_m = _ssd(tok(z), tok(xbc), tok(dt), *lp['ssd'])
        y_s = _s5(tok(us), *lp['s5'])
        y_r = _rwkv(tok(ur), *lp['rwkv'])
        flat = lambda a: a.reshape(n, a.shape[-1])
        xf = _outproj(xf, flat(y_m), flat(y_s), flat(y_r), lp['w_out'])
        xf = _ffn(xf, *lp['ffn2'])
    out = _final_norm(xf, final_norm.reshape(1, d).astype(F32))
    return out.reshape(bsz, seq, d).astype(x.dtype)
```

```python
import functools
import math

import jax
import jax.numpy as jnp
from jax import lax
from jax.experimental import pallas as pl
from jax.experimental.pallas import tpu as pltpu

F32 = jnp.float32
BF16 = jnp.bfloat16

NORM_EPS = 1e-5
RWKV_GN_EPS = 64e-5
MACARON_WEIGHT = 0.5

LANES = 128
SUBLANES = 8
VMEM_LIMIT_BYTES = 56 * 1024 * 1024

D_MODEL = 1024
D_FF = 2816
M_WIDTH = 512
M_HEADDIM = 64
M_HEADS = 8
M_GROUPS = 2
M_STATE = 128
M_CONV = 4
M_CONV_DIM = 1024
S_WIDTH = 256
S_GROUPS = 16
S_GROUP_CH = 16
S_STATE = 64
R_WIDTH = 256
R_HEADS = 4
R_HEADDIM = 64
R_DECAY_LORA = 32
R_AAA_LORA = 32
R_GATE_LORA = 64
R_PROJ = 896
DT_PAD = LANES
U_COLS = M_WIDTH + M_CONV_DIM + DT_PAD + S_WIDTH + R_PROJ

ROW_TILE = 512
FF_CHUNK = 256
SSD_CHUNK = 128
S5_CHUNK = 64
S5_BATCH_PITCH = 72
S5_TIME_PITCH = 24
S5_PROJ_COLS = 256
S5_SCAN_LANES = 512
S5_SCAN_UNROLL = 4
S5_OUT_ROWS = 256
RWKV_CHUNK = 64
RWKV_ROWS = 4
CONV_HALO = SUBLANES


def _mm(a, b):
    return jnp.dot(a.astype(BF16), b.astype(BF16), preferred_element_type=F32)


def _mm_nt(a, b):
    return lax.dot_general(a.astype(BF16), b.astype(BF16), (((1,), (1,)), ((), ())),
                           preferred_element_type=F32)


def _mm_tn(a, b):
    return lax.dot_general(a.astype(BF16), b.astype(BF16), (((0,), (0,)), ((), ())),
                           preferred_element_type=F32)


def _split_mm(x, m, passes, left=False):
    acc = None
    rem = x
    for _ in range(passes):
        hi = rem.astype(BF16)
        p = (jnp.dot(m, hi, preferred_element_type=F32) if left
             else jnp.dot(hi, m, preferred_element_type=F32))
        acc = p if acc is None else acc + p
        rem = rem - hi.astype(F32)
    return acc


def _rmsnorm(x, g, eps=NORM_EPS):
    return x * lax.rsqrt(jnp.mean(x * x, axis=-1, keepdims=True) + eps) * g


def _sigmoid(x):
    return 1.0 / (1.0 + jnp.exp(-x))


def _silu(x):
    return x * _sigmoid(x)


def _softplus(x):
    return jnp.maximum(x, 0.0) + jnp.log(1.0 + jnp.exp(-jnp.abs(x)))


def _iota(shape, dim):
    return lax.broadcasted_iota(jnp.int32, shape, dim)


def _tril_ones(n, dtype=BF16):
    return jnp.where(_iota((n, n), 0) >= _iota((n, n), 1), 1.0, 0.0).astype(dtype)


def _resident(shape):
    nd = len(shape)
    return pl.BlockSpec(shape, lambda *_: (0,) * nd, pipeline_mode=pl.Buffered(1))


def _params(sem):
    return pltpu.CompilerParams(dimension_semantics=sem, vmem_limit_bytes=VMEM_LIMIT_BYTES)


U_WIDTHS = (M_WIDTH, M_CONV_DIM, DT_PAD, S_WIDTH, R_PROJ)


def _rows(c):
    return pl.BlockSpec((ROW_TILE, c), lambda i: (i, 0))


def _ffn_weight_specs():
    return [_resident((1, D_MODEL)), _resident((D_MODEL, D_FF)), _resident((D_MODEL, D_FF)),
            _resident((D_FF, D_MODEL))]


def _ffn_tile(x, g_ref, wg_ref, wu_ref, wd_ref, act_ref):
    h = _rmsnorm(x, g_ref[...]).astype(BF16)
    for j in range(D_FF // FF_CHUNK):
        sl = slice(j * FF_CHUNK, (j + 1) * FF_CHUNK)
        gate = jnp.dot(h, wg_ref[:, sl], preferred_element_type=F32)
        up = jnp.dot(h, wu_ref[:, sl], preferred_element_type=F32)
        act_ref[:, sl] = (_silu(gate) * up).astype(BF16)
    return x + MACARON_WEIGHT * jnp.dot(act_ref[...], wd_ref[...], preferred_element_type=F32)


def _ffn_inproj_kernel(x_ref, g_ref, wg_ref, wu_ref, wd_ref, gmix_ref, win_ref,
                       x1_ref, z_ref, xbc_ref, dt_ref, us_ref, ur_ref, act_ref):
    x1 = _ffn_tile(x_ref[...], g_ref, wg_ref, wu_ref, wd_ref, act_ref)
    x1_ref[...] = x1
    h = _rmsnorm(x1, gmix_ref[...]).astype(BF16)
    o = 0
    for ref in (z_ref, xbc_ref, dt_ref, us_ref, ur_ref):
        w = ref.shape[-1]
        ref[...] = jnp.dot(h, win_ref[:, o:o + w], preferred_element_type=F32)
        o += w


def _ffn_inproj(x, g, wg, wu, wd, gmix, win):
    n = x.shape[0]
    widths = (D_MODEL,) + U_WIDTHS
    return pl.pallas_call(
        _ffn_inproj_kernel,
        out_shape=[jax.ShapeDtypeStruct((n, c), F32) for c in widths],
        grid=(n // ROW_TILE,),
        in_specs=[_rows(D_MODEL)] + _ffn_weight_specs()
        + [_resident((1, D_MODEL)), _resident((D_MODEL, U_COLS))],
        out_specs=[_rows(c) for c in widths],
        scratch_shapes=[pltpu.VMEM((ROW_TILE, D_FF), BF16)],
        compiler_params=_params(("parallel",)),
        name="ffn_inproj",
    )(x, g, wg, wu, wd, gmix, win)


def _outproj_ffn_kernel(x_ref, ym_ref, ys_ref, yr_ref, wo_ref, g_ref, wg_ref, wu_ref, wd_ref,
                        gfin_ref, o_ref, act_ref, *, final):
    y = jnp.dot(ym_ref[...].astype(BF16), wo_ref[0:M_WIDTH, :], preferred_element_type=F32)
    y += jnp.dot(ys_ref[...].astype(BF16), wo_ref[M_WIDTH:M_WIDTH + S_WIDTH, :],
                 preferred_element_type=F32)
    y += jnp.dot(yr_ref[...].astype(BF16), wo_ref[M_WIDTH + S_WIDTH:, :],
                 preferred_element_type=F32)
    x = _ffn_tile(x_ref[...] + y, g_ref, wg_ref, wu_ref, wd_ref, act_ref)
    o_ref[...] = _rmsnorm(x, gfin_ref[...]) if final else x


def _outproj_ffn(x, ym, ys, yr, wo, g, wg, wu, wd, gfin, *, final):
    n = x.shape[0]
    return pl.pallas_call(
        functools.partial(_outproj_ffn_kernel, final=final),
        out_shape=jax.ShapeDtypeStruct((n, D_MODEL), F32),
        grid=(n // ROW_TILE,),
        in_specs=[_rows(D_MODEL), _rows(M_WIDTH), _rows(S_WIDTH), _rows(R_WIDTH),
                  _resident((D_MODEL, D_MODEL))] + _ffn_weight_specs()
        + [_resident((1, D_MODEL))],
        out_specs=_rows(D_MODEL),
        scratch_shapes=[pltpu.VMEM((ROW_TILE, D_FF), BF16)],
        compiler_params=_params(("parallel",)),
        name="outproj_ffn",
    )(x, ym, ys, yr, wo, g, wg, wu, wd, gfin)


def _ssd_kernel(z_ref, xbc_ref, dt_ref, convw_ref, convb_ref, dtb_ref, a_ref, dskip_ref,
                normw_ref, expand_ref, y_ref, win_ref, state_ref):
    T = SSD_CHUNK
    GN = M_STATE
    GW = M_WIDTH // M_GROUPS

    @pl.when(pl.program_id(1) == 0)
    def _():
        win_ref[0:CONV_HALO, :] = jnp.zeros((CONV_HALO, M_CONV_DIM), F32)
        state_ref[...] = jnp.zeros_like(state_ref)

    win_ref[CONV_HALO:CONV_HALO + T, :] = xbc_ref[...]
    acc = jnp.broadcast_to(convb_ref[...], (T, M_CONV_DIM))
    for j in range(M_CONV):
        o = CONV_HALO - (M_CONV - 1) + j
        acc = acc + win_ref[o:o + T, :] * convw_ref[j:j + 1, :]
    win_ref[0:CONV_HALO, :] = win_ref[T:T + CONV_HALO, :]
    xa = _silu(acc)
    xs = xa[:, 0:M_WIDTH]
    bm = xa[:, M_WIDTH:M_WIDTH + M_GROUPS * GN]
    cm = xa[:, M_WIDTH + M_GROUPS * GN:]

    dt = _softplus(dt_ref[...] + dtb_ref[...])
    da = dt * a_ref[...]
    cs = _split_mm(da, _tril_ones(T), 3, left=True)
    cs_t = cs.T
    expand = expand_ref[...]
    dt_x = _split_mm(dt, expand, 2)
    cs_x = _split_mm(cs, expand, 3)
    cs_last_x = cs_x[T - 1:T, :]
    x_dt = xs * dt_x
    x_dec = x_dt * jnp.exp(cs_last_x - cs_x)
    ecs_x = jnp.exp(cs_x)
    state_decay = jnp.exp(cs_last_x)

    causal = _iota((T, T), 0) >= _iota((T, T), 1)
    lane_lo = _iota((T, LANES), 1) < M_HEADDIM
    y_parts = []
    for g in range(M_GROUPS):
        cg = cm[:, g * GN:(g + 1) * GN]
        bg = bm[:, g * GN:(g + 1) * GN]
        cb = _mm_nt(cg, bg)
        for q in range(GW // LANES):
            col = g * GW + q * LANES
            xq = x_dt[:, col:col + LANES]
            outs = []
            for r in range(LANES // M_HEADDIM):
                h = col // M_HEADDIM + r
                seg = cs[:, h:h + 1] - cs_t[h:h + 1, :]
                lmat = jnp.where(causal, jnp.exp(seg), 0.0)
                outs.append(_mm(cb * lmat, xq))
            y_parts.append(jnp.where(lane_lo, outs[0], outs[1]))
    y_diag = jnp.concatenate(y_parts, axis=1)

    y_off_parts = []
    for g in range(M_GROUPS):
        cg = cm[:, g * GN:(g + 1) * GN]
        bg = bm[:, g * GN:(g + 1) * GN]
        st = state_ref[g * GN:(g + 1) * GN, :]
        y_off_parts.append(_mm(cg, st))
        state_ref[g * GN:(g + 1) * GN, :] = (
            st * state_decay[:, g * GW:(g + 1) * GW] + _mm_tn(bg, x_dec[:, g * GW:(g + 1) * GW]))
    y_off = jnp.concatenate(y_off_parts, axis=1) * ecs_x

    y = y_diag + y_off + xs * dskip_ref[...]
    y = y * _silu(z_ref[...])
    y_ref[...] = _rmsnorm(y, normw_ref[...])


def _ssd(z, xbc, dt, convw, convb, dtb, a, dskip, normw, expand):
    bsz, seq, _ = z.shape
    T = SSD_CHUNK
    tok = lambda c: pl.BlockSpec((None, T, c), lambda b, i: (b, i, 0))
    return pl.pallas_call(
        _ssd_kernel,
        out_shape=jax.ShapeDtypeStruct((bsz, seq, M_WIDTH), F32),
        grid=(bsz, seq // T),
        in_specs=[tok(M_WIDTH), tok(M_CONV_DIM), tok(DT_PAD),
                  _resident((M_CONV, M_CONV_DIM)), _resident((1, M_CONV_DIM)),
                  _resident((1, DT_PAD)), _resident((1, DT_PAD)), _resident((1, M_WIDTH)),
                  _resident((1, M_WIDTH)), _resident((DT_PAD, M_WIDTH))],
        out_specs=tok(M_WIDTH),
        scratch_shapes=[pltpu.VMEM((T + CONV_HALO, M_CONV_DIM), F32),
                        pltpu.VMEM((M_GROUPS * M_STATE, M_WIDTH // M_GROUPS), F32)],
        compiler_params=_params(("parallel", "arbitrary")),
        name="ssd",
    )(z, xbc, dt, convw, convb, dtb, a, dskip, normw, expand)


def _s5_kernel(u_ref, wb_ref, ar_ref, ai_ref, wcr_ref, wci_ref, dskip_ref, gluw_ref,
               glub_ref, y_ref, ubm_ref, utm_ref, x_ref, ytm_ref, carry_ref):
    nb, T, _ = u_ref.shape
    NS = S_GROUPS * S_STATE
    lane_tiles = S_WIDTH // LANES

    @pl.when(pl.program_id(0) == 0)
    def _():
        carry_ref[...] = jnp.zeros_like(carry_ref)

    for b in range(nb):
        for j in range(lane_tiles):
            ubm_ref[j, b * S5_BATCH_PITCH:b * S5_BATCH_PITCH + T, :] = (
                u_ref[b, :, j * LANES:(j + 1) * LANES])
    for t in range(T):
        for j in range(lane_tiles):
            utm_ref[t * nb:(t + 1) * nb, j * LANES:(j + 1) * LANES] = (
                ubm_ref[j, pl.ds(t, nb, stride=S5_BATCH_PITCH), :])

    u_tm = utm_ref[...].astype(BF16)
    for n in range(2 * NS // S5_PROJ_COLS):
        cols = slice(n * S5_PROJ_COLS, (n + 1) * S5_PROJ_COLS)
        x_ref[:, cols] = jnp.dot(u_tm, wb_ref[:, cols], preferred_element_type=F32)

    for g in range(NS // S5_SCAN_LANES):
        re = slice(g * S5_SCAN_LANES, (g + 1) * S5_SCAN_LANES)
        im = slice(NS + g * S5_SCAN_LANES, NS + (g + 1) * S5_SCAN_LANES)
        ar = ar_ref[:, re]
        ai = ai_ref[:, re]

        def step(t, x, re=re, im=im, ar=ar, ai=ai):
            xr, xi = x
            rows = pl.ds(pl.multiple_of(t * nb, nb), nb)
            nr = ar * xr - ai * xi + x_ref[rows, re]
            ni = ar * xi + ai * xr + x_ref[rows, im]
            x_ref[rows, re] = nr
            x_ref[rows, im] = ni
            return nr, ni

        xr, xi = lax.fori_loop(0, T, step, (carry_ref[:, re], carry_ref[:, im]),
                               unroll=S5_SCAN_UNROLL)
        carry_ref[:, re] = xr
        carry_ref[:, im] = xi

    steps_per_block = S5_OUT_ROWS // nb
    blocks = [slice(i * S5_OUT_ROWS, (i + 1) * S5_OUT_ROWS) for i in range(T // steps_per_block)]
    ys = [_mm(x_ref[rows, 0:NS], wcr_ref[...]) + _mm(x_ref[rows, NS:], wci_ref[...])
          for rows in blocks]
    ys = [y + dskip_ref[...] * utm_ref[rows, :] for y, rows in zip(ys, blocks)]
    ys = [0.5 * y * (1.0 + jnp.tanh(math.sqrt(2.0 / math.pi) * (y + 0.044715 * (y * y * y))))
          for y in ys]
    gs = [_mm(y, gluw_ref[...]) for y in ys]
    ys = [y * _sigmoid(g + glub_ref[...]) for y, g in zip(ys, gs)]
    for i, y in enumerate(ys):
        for tt in range(steps_per_block):
            o = (i * steps_per_block + tt) * S5_TIME_PITCH
            for j in range(lane_tiles):
                ytm_ref[j, o:o + nb, :] = y[tt * nb:(tt + 1) * nb, j * LANES:(j + 1) * LANES]

    for b in range(nb):
        for j in range(lane_tiles):
            y_ref[b, :, j * LANES:(j + 1) * LANES] = (
                ytm_ref[j, pl.ds(b, T, stride=S5_TIME_PITCH), :])


def _s5(u, wb, ar, ai, wcr, wci, dskip, gluw, glub):
    bsz, seq, _ = u.shape
    T = S5_CHUNK
    NS = S_GROUPS * S_STATE
    assert bsz % SUBLANES == 0 and bsz <= S5_TIME_PITCH and T <= S5_BATCH_PITCH
    assert (T * bsz) % S5_OUT_ROWS == 0 and S5_OUT_ROWS % bsz == 0
    tok = pl.BlockSpec((bsz, T, S_WIDTH), lambda i: (0, i, 0))
    return pl.pallas_call(
        _s5_kernel,
        out_shape=jax.ShapeDtypeStruct((bsz, seq, S_WIDTH), F32),
        grid=(seq // T,),
        in_specs=[tok, _resident((S_WIDTH, 2 * NS)), _resident((bsz, NS)),
                  _resident((bsz, NS)), _resident((NS, S_WIDTH)),
                  _resident((NS, S_WIDTH)), _resident((1, S_WIDTH)),
                  _resident((S_WIDTH, S_WIDTH)), _resident((1, S_WIDTH))],
        out_specs=tok,
        scratch_shapes=[pltpu.VMEM((S_WIDTH // LANES, bsz * S5_BATCH_PITCH, LANES), F32),
                        pltpu.VMEM((T * bsz, S_WIDTH), F32),
                        pltpu.VMEM((T * bsz, 2 * NS), F32),
                        pltpu.VMEM((S_WIDTH // LANES, T * S5_TIME_PITCH, LANES), F32),
                        pltpu.VMEM((bsz, 2 * NS), F32)],
        compiler_params=_params(("arbitrary",)),
        name="s5",
    )(u, wb, ar, ai, wcr, wci, dskip, gluw, glub)


def _rwkv_kernel(u_ref, mu_ref, w0_ref, w2_ref, a0_ref, a2_ref, g2_ref, kk_ref, ka_ref,
                 rk_ref, gnw_ref, gnb_ref, y_ref, prev_ref, state_ref):
    T = RWKV_CHUNK
    W = R_WIDTH
    HT = R_HEADS * T

    @pl.when(pl.program_id(1) == 0)
    def _():
        prev_ref[...] = jnp.zeros_like(prev_ref)
        state_ref[...] = jnp.zeros_like(state_ref)

    head_of_lane = _iota((W, W), 0) // R_HEADDIM == _iota((W, W), 1) // R_HEADDIM
    head_ones = jnp.where(head_of_lane, 1.0, 0.0).astype(BF16)
    tril = _tril_ones(T)
    lane_head = _iota((T, W), 1) // R_HEADDIM
    row = _iota((HT, HT), 0)
    col = _iota((HT, HT), 1)
    same_head = row // T == col // T
    strict = same_head & (row % T > col % T)
    lower = same_head & (row % T >= col % T)
    diag = row == col
    first_row = _iota((T, R_PROJ), 0) == 0

    def stack(x):
        return jnp.concatenate([jnp.where(lane_head == h, x, 0.0) for h in range(R_HEADS)],
                               axis=0)

    def rep(x):
        return jnp.concatenate([x] * R_HEADS, axis=0)

    def unstack(x):
        out = x[0:T]
        for h in range(1, R_HEADS):
            out = out + x[h * T:(h + 1) * T]
        return out

    def mix(d, b):
        u = u_ref[b]
        shifted = jnp.where(first_row, prev_ref[b, 0:1, :], pltpu.roll(u, 1, axis=0))
        prev_ref[b] = jnp.broadcast_to(u[T - 1:T, :], (SUBLANES, R_PROJ))
        u = u + (shifted - u) * mu_ref[...]
        d.r, d.k, d.v = u[:, 0:W], u[:, W:2 * W], u[:, 2 * W:3 * W]
        lora = u[:, 3 * W:]
        w_log = -_softplus(-(w0_ref[...] + _mm(jnp.tanh(lora), w2_ref[...]))) - 0.5
        d.logw = -jnp.exp(w_log)
        d.a = _sigmoid(a0_ref[...] + _mm(lora, a2_ref[...]))
        d.gate = _mm(_sigmoid(lora), g2_ref[...])
        d.kk = d.k * kk_ref[...]
        d.cum = _split_mm(d.logw, tril, 3, left=True)
        d.kk_ss = _split_mm(d.kk * d.kk, head_ones, 2)

    def grams(d, b):
        kk = d.kk / jnp.maximum(jnp.sqrt(d.kk_ss), 1e-12)
        k = d.k * (1.0 + (d.a - 1.0) * ka_ref[...])
        cum = d.cum
        d.cum_end = cum[T - 1:T, :]
        g_inv = jnp.exp(-cum)
        g_tail = jnp.exp(d.cum_end - cum)
        kka = kk * d.a
        d.k = k
        d.b_h = kka * g_tail
        d.k_h = k * g_tail
        d.sa = stack(-kk * jnp.exp(cum - d.logw))
        d.sr = stack(d.r * jnp.exp(cum))
        d.sv = stack(d.v)
        rb = rep(kka * g_inv)
        rk = rep(k * g_inv)
        d.n_ab = jnp.where(strict, _mm_nt(d.sa, rb), 0.0)
        d.m_ak = jnp.where(strict, _mm_nt(d.sa, rk), 0.0)
        d.m_rb = jnp.where(lower, _mm_nt(d.sr, rb), 0.0)
        d.m_rk = jnp.where(lower, _mm_nt(d.sr, rk), 0.0)
        d.p = d.n_ab
        d.inv = jnp.where(diag, 1.0, d.n_ab)

    def rhs(d, b):
        d.rhs = jnp.concatenate([d.sa, _mm(d.m_ak, d.sv)], axis=1)
        d.y0 = _mm(d.m_rk, d.sv)

    def square(d, b):
        d.p = _mm(d.p, d.p)

    def extend(d, b):
        d.inv = d.inv + _mm(d.p, d.inv)

    def solve(d, b):
        d.x = _mm(d.inv, d.rhs)

    def project(d, b):
        x = _mm(d.m_rb, d.x)
        d.q = unstack(d.sr + x[:, 0:W])
        d.y0 = unstack(d.y0 + x[:, W:])
        d.w_u = unstack(d.x[:, 0:W])
        d.u0 = unstack(d.x[:, W:])

    def apply_state(d, b):
        d.s0 = state_ref[b]
        d.y = _mm_nt(d.q, d.s0) + d.y0
        d.p_t = jnp.where(head_of_lane, _mm_tn(d.w_u, d.b_h), 0.0)
        d.z_t = jnp.where(head_of_lane,
                          _mm_tn(jnp.concatenate([d.v, d.u0], axis=0),
                                 jnp.concatenate([d.k_h, d.b_h], axis=0)), 0.0)

    def update_state(d, b):
        state_ref[b] = d.s0 * jnp.exp(d.cum_end) + _mm(d.s0, d.p_t) + d.z_t
        d.mean = _split_mm(d.y, head_ones, 2) * (1.0 / R_HEADDIM)
        d.bonus = _split_mm(d.r * d.k * rk_ref[...], head_ones, 2) * d.v

    def variance(d, b):
        d.yc = d.y - d.mean
        d.var = _split_mm(d.yc * d.yc, head_ones, 2) * (1.0 / R_HEADDIM)

    def finish(d, b):
        yn = d.yc * lax.rsqrt(d.var + RWKV_GN_EPS) * gnw_ref[...] + gnb_ref[...]
        y_ref[b] = (yn + d.bonus) * d.gate

    stages = [mix, grams, rhs]
    for _ in range(int(math.log2(T)) - 1):
        stages += [square, extend]
    stages += [solve, project, apply_state, update_state, variance, finish]
    rows = [_Vals() for _ in range(RWKV_ROWS)]
    for stage in stages:
        for b, d in enumerate(rows):
            stage(d, b)


class _Vals:
    pass


def _rwkv(u, mu, w0, w2, a0, a2, g2, kk, ka, rk, gnw, gnb):
    bsz, seq, _ = u.shape
    T = RWKV_CHUNK
    vec = _resident((1, R_WIDTH))
    lora = _resident((R_PROJ - 3 * R_WIDTH, R_WIDTH))
    return pl.pallas_call(
        _rwkv_kernel,
        out_shape=jax.ShapeDtypeStruct((bsz, seq, R_WIDTH), F32),
        grid=(bsz // RWKV_ROWS, seq // T),
        in_specs=[pl.BlockSpec((RWKV_ROWS, T, R_PROJ), lambda b, i: (b, i, 0)),
                  _resident((1, R_PROJ)), vec, lora, vec, lora, lora, vec, vec, vec, vec, vec],
        out_specs=pl.BlockSpec((RWKV_ROWS, T, R_WIDTH), lambda b, i: (b, i, 0)),
        scratch_shapes=[pltpu.VMEM((RWKV_ROWS, SUBLANES, R_PROJ), F32),
                        pltpu.VMEM((RWKV_ROWS, R_WIDTH, R_WIDTH), F32)],
        compiler_params=_params(("parallel", "arbitrary")),
        name="rwkv7",
    )(u, mu, w0, w2, a0, a2, g2, kk, ka, rk, gnw, gnb)


def _pad_cols(a, n):
    return jnp.pad(a, ((0, 0), (0, n - a.shape[1])))


def _s5_discretize(A_re, A_im, B_re, B_im, C_re, C_im, log_dt, bsz):
    dt = jnp.exp(log_dt)[:, None]
    mag = jnp.exp(A_re * dt)
    abar_re, abar_im = mag * jnp.cos(A_im * dt), mag * jnp.sin(A_im * dt)
    den = A_re * A_re + A_im * A_im
    nr, ni = abar_re - 1.0, abar_im
    coef_re = (nr * A_re + ni * A_im) / den
    coef_im = (ni * A_re - nr * A_im) / den
    bb_re = coef_re[..., None] * B_re - coef_im[..., None] * B_im
    bb_im = coef_re[..., None] * B_im + coef_im[..., None] * B_re
    eye = jnp.eye(S_GROUPS, dtype=F32)
    ns = S_GROUPS * S_STATE
    in_bd = lambda m: jnp.einsum('gph,gk->ghkp', m, eye).reshape(S_WIDTH, ns)
    out_bd = lambda m: jnp.einsum('ghp,gk->gpkh', m, eye).reshape(ns, S_WIDTH)
    wb = jnp.concatenate([in_bd(bb_re), in_bd(bb_im)], axis=1).astype(BF16)
    wcr = out_bd(C_re).astype(BF16)
    wci = (-out_bd(C_im)).astype(BF16)
    on_rows = lambda m: jnp.broadcast_to(m.reshape(1, ns), (bsz, ns))
    return wb, on_rows(abar_re), on_rows(abar_im), wcr, wci


def _layer_params(i, p, bsz):
    row = lambda a: a.reshape(1, -1).astype(F32)
    w_in = p['w_in'][i]
    m_proj = M_WIDTH + M_CONV_DIM
    w_u = jnp.concatenate([
        w_in[:, :m_proj],
        _pad_cols(w_in[:, m_proj:m_proj + M_HEADS], DT_PAD),
        w_in[:, m_proj + M_HEADS:]], axis=1).astype(BF16)
    head_lanes = jnp.repeat(jnp.eye(M_HEADS, dtype=F32), M_HEADDIM, axis=1)
    lora_rows = R_PROJ - 3 * R_WIDTH
    lora_pad = lambda m, o: jnp.pad(m, ((o, lora_rows - o - m.shape[0]), (0, 0))).astype(BF16)
    return dict(
        ffn1=(row(p['ffn1_norm'][i]), p['ffn1_wg'][i].astype(BF16), p['ffn1_wu'][i].astype(BF16),
              p['ffn1_wd'][i].astype(BF16)),
        ffn2=(row(p['ffn2_norm'][i]), p['ffn2_wg'][i].astype(BF16), p['ffn2_wu'][i].astype(BF16),
              p['ffn2_wd'][i].astype(BF16)),
        mix_norm=row(p['mix_norm'][i]),
        w_u=w_u,
        w_out=p['w_out'][i].astype(BF16),
        ssd=(p['m_conv_w'][i].T.astype(F32), row(p['m_conv_b'][i]),
             _pad_cols(row(p['m_dt_bias'][i]), DT_PAD),
             _pad_cols(row(-jnp.exp(p['m_A_log'][i].astype(F32))), DT_PAD),
             row(jnp.repeat(p['m_D'][i], M_HEADDIM)), row(p['m_norm_w'][i]),
             jnp.pad(head_lanes, ((0, DT_PAD - M_HEADS), (0, 0))).astype(BF16)),
        s5=_s5_discretize(p['s_A_re'][i].astype(F32), p['s_A_im'][i].astype(F32),
                          p['s_B_re'][i].astype(F32), p['s_B_im'][i].astype(F32),
                          p['s_C_re'][i], p['s_C_im'][i], p['s_log_dt'][i].astype(F32), bsz)
        + (row(p['s_D'][i]), p['s_glu_w'][i].astype(BF16), row(p['s_glu_b'][i])),
        rwkv=(row(p['r_mu'][i]), row(p['r_w0'][i]), lora_pad(p['r_w2'][i], 0),
              row(p['r_a0'][i]), lora_pad(p['r_a2'][i], R_DECAY_LORA),
              lora_pad(p['r_g2'][i], R_DECAY_LORA + R_AAA_LORA),
              row(p['r_k_k'][i]), row(p['r_k_a'][i]), row(p['r_r_k'][i]),
              row(p['r_gn_w'][i]), row(p['r_gn_b'][i])),
    )


def kernel(x, ffn1_norm, ffn1_wg, ffn1_wu, ffn1_wd, mix_norm, w_in, w_out, m_A_log, m_dt_bias, m_conv_w, m_conv_b, m_D, m_norm_w, s_A_re, s_A_im, s_B_re, s_B_im, s_C_re, s_C_im, s_log_dt, s_D, s_glu_w, s_glu_b, r_mu, r_w0, r_w2, r_a0, r_a2, r_g2, r_k_k, r_k_a, r_r_k, r_gn_w, r_gn_b, ffn2_norm, ffn2_wg, ffn2_wu, ffn2_wd, final_norm):
    p = dict(ffn1_norm=ffn1_norm, ffn1_wg=ffn1_wg, ffn1_wu=ffn1_wu, ffn1_wd=ffn1_wd,
             mix_norm=mix_norm, w_in=w_in, w_out=w_out, m_A_log=m_A_log, m_dt_bias=m_dt_bias,
             m_conv_w=m_conv_w, m_conv_b=m_conv_b, m_D=m_D, m_norm_w=m_norm_w,
             s_A_re=s_A_re, s_A_im=s_A_im, s_B_re=s_B_re, s_B_im=s_B_im, s_C_re=s_C_re,
             s_C_im=s_C_im, s_log_dt=s_log_dt, s_D=s_D, s_glu_w=s_glu_w, s_glu_b=s_glu_b,
             r_mu=r_mu, r_w0=r_w0, r_w2=r_w2, r_a0=r_a0, r_a2=r_a2, r_g2=r_g2, r_k_k=r_k_k,
             r_k_a=r_k_a, r_r_k=r_r_k, r_gn_w=r_gn_w, r_gn_b=r_gn_b, ffn2_norm=ffn2_norm,
             ffn2_wg=ffn2_wg, ffn2_wu=ffn2_wu, ffn2_wd=ffn2_wd)
    bsz, seq, d = x.shape
    assert d == D_MODEL and (bsz * seq) % ROW_TILE == 0
    assert seq % SSD_CHUNK == 0 and seq % S5_CHUNK == 0 and seq % RWKV_CHUNK == 0
    n = bsz * seq
    depth = w_in.shape[0]
    assert depth >= 1 and bsz % RWKV_ROWS == 0
    xf = x.reshape(n, d).astype(F32)
    gfin = final_norm.reshape(1, d).astype(F32)
    for i in range(depth):
        lp = _layer_params(i, p, bsz)
        xf, z, xbc, dt, us, ur = _ffn_inproj(xf, *lp['ffn1'], lp['mix_norm'], lp['w_u'])
        tok = lambda a: a.reshape(bsz, seq, a.shape[-1])
        y_m = _ssd(tok(z), tok(xbc), tok(dt), *lp['ssd'])
        y_s = _s5(tok(us), *lp['s5'])
        y_r = _rwkv(tok(ur), *lp['rwkv'])
        flat = lambda a: a.reshape(n, a.shape[-1])
        xf = _outproj_ffn(xf, flat(y_m), flat(y_s), flat(y_r), lp['w_out'], *lp['ffn2'], gfin,
                          final=(i == depth - 1))
    return xf.reshape(bsz, seq, d).astype(x.dtype)
```
